```python
import math
import jax, jax.numpy as jnp
from jax import lax
import numpy as np

D_MODEL = 1024
BATCH = 4
SEQ = 4096
DEPTH = 1
DEC_BATCH = 32
DEC_SEQ = 8
PAST_LEN = 8192
PAGE_SIZE = 128

N_META = 16
HA = 4
DQK = 64
DVA = 128
HB = 4
DKB = 128
DVB = 128
CONV_W = 4
CHUNK = 64
QA_W = HA * 2 * DQK
KA_W = HA * 2 * DQK
VA_W = HA * DVA
CONV_DIM = HB * DKB * 2 + HB * DVB
Z_W = HB * DVB
PROJ_W = QA_W + KA_W + VA_W + CONV_DIM + Z_W + 2 * HB
MIX_W = HA * DVA + HB * DVB
HP = 8
NK = 128
N_EXPERTS = NK * NK
DPK = 256
PEER_TOPK = 16
PEER_BLOCK = 128
Q_BLOCK = 128
EPS = 1e-6

kernel_name = 'hymba_diffattn_gdn_peer_step'


def rmsnorm(x, g):
    xf = x.astype(jnp.float32)
    y = xf * lax.rsqrt(jnp.mean(xf * xf, axis=-1, keepdims=True) + EPS)
    return (y * g.astype(jnp.float32)).astype(x.dtype)


def l2norm(x):
    xf = x.astype(jnp.float32)
    return xf * lax.rsqrt(jnp.sum(xf * xf, axis=-1, keepdims=True) + EPS)


def project(xn, w_in, q_norm, k_norm):
    p = xn @ w_in
    cuts = np.cumsum([QA_W, KA_W, VA_W, CONV_DIM, Z_W, HB]).tolist()
    qa, ka, va, qkv_b, z, b_raw, a_raw = jnp.split(p, cuts, axis=-1)
    lead = xn.shape[:-1]
    q = rmsnorm(qa.reshape(lead + (HA, 2, DQK)), q_norm)
    k = rmsnorm(ka.reshape(lead + (HA, 2, DQK)), k_norm)
    v = va.reshape(lead + (HA, DVA))
    return q, k, v, qkv_b, z, b_raw, a_raw


def diff_attn_core(q, k, v, qpos, kpos, kvalid, lam, slopes):
    s = jnp.einsum('bqhcd,bkhcd->bhcqk', q.astype(jnp.float32), k.astype(jnp.float32)) * (DQK ** -0.5)
    dist = jnp.abs(qpos[:, None] - kpos[None, :]).astype(jnp.float32)
    s = s - slopes[:, None, None, None] * dist
    mask = (kpos[None, :] <= qpos[:, None]) & kvalid[None, :]
    p = jax.nn.softmax(jnp.where(mask, s, -jnp.inf), axis=-1)
    w = p[:, :, 0] - lam * p[:, :, 1]
    return jnp.einsum('bhqk,bkhd->bqhd', w, v.astype(jnp.float32))


def diff_attn_prompt(q, k, v, lam, slopes):
    b, t = q.shape[:2]
    nb = -(-t // Q_BLOCK)
    tp = nb * Q_BLOCK
    qp = jnp.pad(q, ((0, 0), (0, tp - t), (0, 0), (0, 0), (0, 0)))
    kp = jnp.pad(k, ((0, 0), (0, tp - t), (0, 0), (0, 0), (0, 0)))
    vp = jnp.pad(v, ((0, 0), (0, tp - t), (0, 0), (0, 0)))
    kpos = jnp.arange(tp)
    kvalid = kpos < t
    qblocks = jnp.moveaxis(qp.reshape((b, nb, Q_BLOCK) + q.shape[2:]), 1, 0)

    def one_block(args):
        i, qb = args
        qpos = i * Q_BLOCK + jnp.arange(Q_BLOCK)
        return diff_attn_core(qb, kp, vp, qpos, kpos, kvalid, lam, slopes)

    o = lax.map(one_block, (jnp.arange(nb), qblocks))
    o = jnp.moveaxis(o, 0, 1).reshape(b, tp, HA, DVA)
    return o[:, :t]


def gdn_inputs(qkv_pre, conv_prev, conv_w, b_raw, a_raw, a_log, dt_bias):
    t = qkv_pre.shape[1]
    xpad = jnp.concatenate([conv_prev.astype(qkv_pre.dtype), qkv_pre], axis=1)
    conv = sum(conv_w[j] * xpad[:, j:j + t] for j in range(CONV_W))
    conv = jax.nn.silu(conv)
    qb, kb, vb = jnp.split(conv, [HB * DKB, 2 * HB * DKB], axis=-1)
    lead = qb.shape[:2]
    q = l2norm(qb.reshape(lead + (HB, DKB))) * (DKB ** -0.5)
    k = l2norm(kb.reshape(lead + (HB, DKB)))
    v = vb.reshape(lead + (HB, DVB)).astype(jnp.float32)
    beta = jax.nn.sigmoid(b_raw.astype(jnp.float32))
    g = -jnp.exp(a_log.astype(jnp.float32)) * jax.nn.softplus(a_raw.astype(jnp.float32) + dt_bias.astype(jnp.float32))
    return q, k, v, g, beta, xpad[:, t:]


def gdn_chunk(S, q, k, v, g, beta):
    q, k, v = (jnp.swapaxes(a, 1, 2) for a in (q, k, v))
    g = jnp.swapaxes(g, 1, 2)
    beta = jnp.swapaxes(beta, 1, 2)
    c = q.shape[2]
    G = jnp.cumsum(g, axis=-1)
    ti = jnp.arange(c)
    incl = ti[:, None] >= ti[None, :]
    strict = ti[:, None] > ti[None, :]
    decay = jnp.where(incl, jnp.exp(jnp.where(incl, G[..., :, None] - G[..., None, :], 0.0)), 0.0)
    kk = jnp.einsum('bhtd,bhsd->bhts', k, k)
    A = jnp.where(strict, beta[..., :, None] * decay * kk, 0.0)
    eG = jnp.exp(G)
    rhs = beta[..., None] * (v - eG[..., None] * jnp.einsum('bhtk,bhvk->bhtv', k, S))
    L = A + jnp.eye(c, dtype=A.dtype)
    delta = lax.linalg.triangular_solve(L, rhs, left_side=True, lower=True, unit_diagonal=True)
    qk = jnp.einsum('bhtd,bhsd->bhts', q, k)
    o = eG[..., None] * jnp.einsum('bhtk,bhvk->bhtv', q, S) + jnp.einsum('bhts,bhsv->bhtv', decay * qk, delta)
    Gl = G[..., -1:]
    S_new = jnp.exp(Gl)[..., None] * S + jnp.einsum('bhsv,bhsk->bhvk', delta, jnp.exp(Gl - G)[..., None] * k)
    return S_new, jnp.swapaxes(o, 1, 2)


def gdn_prompt(q, k, v, g, beta):
    b, t = q.shape[:2]
    n_chunks = (t - N_META) // CHUNK
    s0 = jnp.zeros((b, HB, DVB, DKB), jnp.float32)
    s1, o_meta = gdn_chunk(s0, q[:, :N_META], k[:, :N_META], v[:, :N_META], g[:, :N_META], beta[:, :N_META])

    def to_chunks(a):
        a = a[:, N_META:]
        return jnp.moveaxis(a.reshape((b, n_chunks, CHUNK) + a.shape[2:]), 1, 0)

    def step(s, xs):
        return gdn_chunk(s, *xs)

    s_fin, o_rest = lax.scan(step, s1, (to_chunks(q), to_chunks(k), to_chunks(v), to_chunks(g), to_chunks(beta)))
    o_rest = jnp.moveaxis(o_rest, 0, 1).reshape(b, n_chunks * CHUNK, HB, DVB)
    return jnp.concatenate([o_meta, o_rest], axis=1), s_fin


def merge_heads(oa, ob, z, lam_init, attn_subln, gdn_norm, w_out, dtype):
    b, t = oa.shape[:2]
    ya = rmsnorm(oa, attn_subln) * (1.0 - lam_init)
    yb = rmsnorm(ob, gdn_norm) * jax.nn.silu(z.reshape(b, t, HB, DVB).astype(jnp.float32))
    y = jnp.concatenate([ya.reshape(b, t, HA * DVA), yb.reshape(b, t, HB * DVB)], axis=-1).astype(dtype)
    return y @ w_out


def peer_ffn(xn, wq, keys, u, v):
    lead = xn.shape[:-1]
    xf = xn.reshape(-1, D_MODEL)
    n = xf.shape[0]
    nb = -(-n // PEER_BLOCK)
    xf = jnp.pad(xf, ((0, nb * PEER_BLOCK - n), (0, 0))).reshape(nb, PEER_BLOCK, D_MODEL)

    def block(xb):
        q = (xb @ wq).reshape(-1, HP, 2, DPK // 2)
        s = jnp.einsum('thcd,hcnd->thcn', q, keys).astype(jnp.float32)
        s1, i1 = lax.top_k(s[:, :, 0], PEER_TOPK)
        s2, i2 = lax.top_k(s[:, :, 1], PEER_TOPK)
        cand = (s1[..., :, None] + s2[..., None, :]).reshape(-1, HP, PEER_TOPK * PEER_TOPK)
        sc, ci = lax.top_k(cand, PEER_TOPK)
        e = jnp.take_along_axis(i1, ci // PEER_TOPK, axis=-1) * NK + jnp.take_along_axis(i2, ci % PEER_TOPK, axis=-1)
        gate = jax.nn.softmax(sc, axis=-1)
        h = jnp.einsum('td,thkd->thk', xb, u[e]).astype(jnp.float32)
        w = (gate * jax.nn.gelu(h, approximate=False)).astype(xb.dtype)
        return jnp.einsum('thk,thkd->td', w, v[e])

    y = lax.map(block, xf).reshape(nb * PEER_BLOCK, D_MODEL)[:n]
    return y.reshape(lead + (D_MODEL,))


def setup_inputs(seed: int = 0) -> dict:
    key = jax.random.key(seed)
    ks = jax.random.split(key, 32)
    f32 = jnp.float32

    def nrm(k, shape, s):
        return jax.random.normal(k, shape, f32) * s

    n_pages = PAST_LEN // PAGE_SIZE
    n_used = DEC_BATCH * n_pages
    n_pool = n_used + (n_used + 3) // 4
    page_table = jax.random.permutation(ks[6], n_pool)[:n_used].reshape(DEC_BATCH, n_pages).astype(jnp.int32)
    a_log = jnp.log(jax.random.uniform(ks[16], (DEPTH, HB), f32, 1.0, 16.0))
    dt = jnp.exp(jax.random.uniform(ks[17], (DEPTH, HB), f32, math.log(1e-3), math.log(1e-1)))
    dt_bias = dt + jnp.log(-jnp.expm1(-dt))
    return {
        'x_prompt': nrm(ks[0], (BATCH, SEQ, D_MODEL), 1.0),
        'x_sample': nrm(ks[1], (DEC_BATCH, DEC_SEQ, D_MODEL), 1.0),
        'cache_k': nrm(ks[2], (DEPTH, n_pool, PAGE_SIZE, HA, 2 * DQK), 1.0),
        'cache_v': nrm(ks[3], (DEPTH, n_pool, PAGE_SIZE, HA, DVA), 1.0),
        'state_ssm': nrm(ks[4], (DEPTH, DEC_BATCH, HB, DVB, DKB), 0.1),
        'state_conv': nrm(ks[5], (DEPTH, DEC_BATCH, CONV_W - 1, CONV_DIM), 1.0),
        'page_table': page_table,
        'meta_tokens': nrm(ks[7], (N_META, D_MODEL), 1.0),
        'norm1': 1.0 + nrm(ks[8], (DEPTH, D_MODEL), 0.05),
        'w_in': nrm(ks[9], (DEPTH, D_MODEL, PROJ_W), D_MODEL ** -0.5),
        'conv_w': nrm(ks[10], (DEPTH, CONV_W, CONV_DIM), CONV_W ** -0.5),
        'q_norm': 1.0 + nrm(ks[11], (DEPTH, DQK), 0.05),
        'k_norm': 1.0 + nrm(ks[12], (DEPTH, DQK), 0.05),
        'lam_q1': nrm(ks[13], (DEPTH, DQK), 0.1),
        'lam_k1': nrm(ks[14], (DEPTH, DQK), 0.1),
        'lam_q2': nrm(ks[15], (DEPTH, DQK), 0.1),
        'lam_k2': nrm(ks[18], (DEPTH, DQK), 0.1),
        'attn_subln': 1.0 + nrm(ks[19], (DEPTH, DVA), 0.05),
        'a_log': a_log,
        'dt_bias': dt_bias,
        'gdn_norm': 1.0 + nrm(ks[20], (DEPTH, DVB), 0.05),
        'w_out': nrm(ks[21], (DEPTH, MIX_W, D_MODEL), MIX_W ** -0.5),
        'norm2': 1.0 + nrm(ks[22], (DEPTH, D_MODEL), 0.05),
        'peer_wq': nrm(ks[23], (DEPTH, D_MODEL, HP * DPK), D_MODEL ** -0.5),
        'peer_keys': nrm(ks[24], (DEPTH, HP, 2, NK, DPK // 2), (DPK // 2) ** -0.5),
        'peer_u': nrm(ks[25], (DEPTH, N_EXPERTS, D_MODEL), D_MODEL ** -0.5),
        'peer_v': nrm(ks[26], (DEPTH, N_EXPERTS, D_MODEL), 0.3),
    }


def reference(x_prompt, x_sample, cache_k, cache_v, state_ssm, state_conv, page_table, meta_tokens,
              norm1, w_in, conv_w, q_norm, k_norm, lam_q1, lam_k1, lam_q2, lam_k2, attn_subln,
              a_log, dt_bias, gdn_norm, w_out, norm2, peer_wq, peer_keys, peer_u, peer_v):
    f32 = jnp.float32
    slopes = jnp.exp2(-(8.0 / HA) * jnp.arange(1, HA + 1, dtype=f32))
    b = x_prompt.shape[0]
    db, ds = x_sample.shape[:2]
    meta = jnp.broadcast_to(meta_tokens.astype(x_prompt.dtype)[None], (b, N_META, D_MODEL))
    hp = jnp.concatenate([meta, x_prompt], axis=1)
    hs = x_sample
    t = hp.shape[1]
    n_past = page_table.shape[1] * PAGE_SIZE
    k_p, v_p, k_s, v_s, ssm_p, ssm_s, conv_p, conv_s = [], [], [], [], [], [], [], []
    for l in range(DEPTH):
        lam_init = 0.8 - 0.6 * math.exp(-0.3 * l)
        lam = (jnp.exp(jnp.sum(lam_q1[l].astype(f32) * lam_k1[l].astype(f32)))
               - jnp.exp(jnp.sum(lam_q2[l].astype(f32) * lam_k2[l].astype(f32))) + lam_init)

        xn = rmsnorm(hp, norm1[l])
        qa, ka, va, qkv_b, z, b_raw, a_raw = project(xn, w_in[l], q_norm[l], k_norm[l])
        oa = diff_attn_prompt(qa, ka, va, lam, slopes)
        conv0 = jnp.zeros((b, CONV_W - 1, CONV_DIM), hp.dtype)
        gq, gk, gv, gg, gbeta, conv_new = gdn_inputs(qkv_b, conv0, conv_w[l], b_raw, a_raw, a_log[l], dt_bias[l])
        ob, s_new = gdn_prompt(gq, gk, gv, gg, gbeta)
        hp = hp + merge_heads(oa, ob, z, lam_init, attn_subln[l], gdn_norm[l], w_out[l], hp.dtype)
        hp = hp + peer_ffn(rmsnorm(hp, norm2[l]), peer_wq[l], peer_keys[l], peer_u[l], peer_v[l])
        k_p.append(ka.reshape(b, t, HA, 2 * DQK))
        v_p.append(va)
        ssm_p.append(s_new)
        conv_p.append(conv_new)

        xn = rmsnorm(hs, norm1[l])
        qa, ka, va, qkv_b, z, b_raw, a_raw = project(xn, w_in[l], q_norm[l], k_norm[l])
        kpast = cache_k[l][page_table].reshape(db, n_past, HA, 2, DQK)
        vpast = cache_v[l][page_table].reshape(db, n_past, HA, DVA)
        k_all = jnp.concatenate([kpast.astype(ka.dtype), ka], axis=1)
        v_all = jnp.concatenate([vpast.astype(va.dtype), va], axis=1)
        qpos = n_past + jnp.arange(ds)
        kpos = jnp.arange(n_past + ds)
        oa = diff_attn_core(qa, k_all, v_all, qpos, kpos, jnp.ones((n_past + ds,), bool), lam, slopes)
        gq, gk, gv, gg, gbeta, conv_new = gdn_inputs(qkv_b, state_conv[l], conv_w[l], b_raw, a_raw, a_log[l], dt_bias[l])
        s_new, ob = gdn_chunk(state_ssm[l].astype(f32), gq, gk, gv, gg, gbeta)
        hs = hs + merge_heads(oa, ob, z, lam_init, attn_subln[l], gdn_norm[l], w_out[l], hs.dtype)
        hs = hs + peer_ffn(rmsnorm(hs, norm2[l]), peer_wq[l], peer_keys[l], peer_u[l], peer_v[l])
        k_s.append(ka.reshape(db, ds, HA, 2 * DQK))
        v_s.append(va)
        ssm_s.append(s_new)
        conv_s.append(conv_new)

    y_prompt = hp[:, N_META:]
    y_sample = hs
    return (y_prompt, y_sample, jnp.stack(k_p), jnp.stack(v_p), jnp.stack(k_s), jnp.stack(v_s),
            jnp.stack(ssm_p), jnp.stack(ssm_s), jnp.stack(conv_p), jnp.stack(conv_s))
```

```python
import math

import jax
import jax.numpy as jnp
import numpy as np
from jax import lax
from jax.experimental import pallas as pl
from jax.experimental.pallas import tpu as pltpu

D_MODEL = 1024
DEPTH = 1
PAGE_SIZE = 128

N_META = 16
HA = 4
DQK = 64
DVA = 128
HB = 4
DKB = 128
DVB = 128
CONV_W = 4
CHUNK = 64
QA_W = HA * 2 * DQK
KA_W = HA * 2 * DQK
VA_W = HA * DVA
CONV_DIM = HB * DKB * 2 + HB * DVB
Z_W = HB * DVB
PROJ_W = QA_W + KA_W + VA_W + CONV_DIM + Z_W + 2 * HB
MIX_W = HA * DVA + HB * DVB
HP = 8
NK = 128
N_EXPERTS = NK * NK
DPK = 256
PEER_TOPK = 16
Q_BLOCK = 128
EPS = 1e-6

F32 = jnp.float32
BF16 = jnp.bfloat16

PEER_TT = 256
PEER_TE = 1024
PEER_ROWS = PEER_TE // NK
NCAND = 80

_NT = (((1,), (1,)), ((), ()))


def rmsnorm(x, g):
    xf = x.astype(jnp.float32)
    y = xf * lax.rsqrt(jnp.mean(xf * xf, axis=-1, keepdims=True) + EPS)
    return (y * g.astype(jnp.float32)).astype(x.dtype)


def l2norm(x):
    xf = x.astype(jnp.float32)
    return xf * lax.rsqrt(jnp.sum(xf * xf, axis=-1, keepdims=True) + EPS)


def project(xn, w_in, q_norm, k_norm):
    p = xn @ w_in
    cuts = np.cumsum([QA_W, KA_W, VA_W, CONV_DIM, Z_W, HB]).tolist()
    qa, ka, va, qkv_b, z, b_raw, a_raw = jnp.split(p, cuts, axis=-1)
    lead = xn.shape[:-1]
    q = rmsnorm(qa.reshape(lead + (HA, 2, DQK)), q_norm)
    k = rmsnorm(ka.reshape(lead + (HA, 2, DQK)), k_norm)
    v = va.reshape(lead + (HA, DVA))
    return q, k, v, qkv_b, z, b_raw, a_raw


def diff_attn_core(q, k, v, qpos, kpos, kvalid, lam, slopes):
    s = jnp.einsum('bqhcd,bkhcd->bhcqk', q.astype(jnp.float32), k.astype(jnp.float32)) * (DQK ** -0.5)
    dist = jnp.abs(qpos[:, None] - kpos[None, :]).astype(jnp.float32)
    s = s - slopes[:, None, None, None] * dist
    mask = (kpos[None, :] <= qpos[:, None]) & kvalid[None, :]
    p = jax.nn.softmax(jnp.where(mask, s, -jnp.inf), axis=-1)
    w = p[:, :, 0] - lam * p[:, :, 1]
    return jnp.einsum('bhqk,bkhd->bqhd', w, v.astype(jnp.float32))


def diff_attn_prompt(q, k, v, lam, slopes):
    b, t = q.shape[:2]
    nb = -(-t // Q_BLOCK)
    tp = nb * Q_BLOCK
    qp = jnp.pad(q, ((0, 0), (0, tp - t), (0, 0), (0, 0), (0, 0)))
    kp = jnp.pad(k, ((0, 0), (0, tp - t), (0, 0), (0, 0), (0, 0)))
    vp = jnp.pad(v, ((0, 0), (0, tp - t), (0, 0), (0, 0)))
    kpos = jnp.arange(tp)
    kvalid = kpos < t
    qblocks = jnp.moveaxis(qp.reshape((b, nb, Q_BLOCK) + q.shape[2:]), 1, 0)

    def one_block(args):
        i, qb = args
        qpos = i * Q_BLOCK + jnp.arange(Q_BLOCK)
        return diff_attn_core(qb, kp, vp, qpos, kpos, kvalid, lam, slopes)

    o = lax.map(one_block, (jnp.arange(nb), qblocks))
    o = jnp.moveaxis(o, 0, 1).reshape(b, tp, HA, DVA)
    return o[:, :t]


def gdn_inputs(qkv_pre, conv_prev, conv_w, b_raw, a_raw, a_log, dt_bias):
    t = qkv_pre.shape[1]
    xpad = jnp.concatenate([conv_prev.astype(qkv_pre.dtype), qkv_pre], axis=1)
    conv = sum(conv_w[j] * xpad[:, j:j + t] for j in range(CONV_W))
    conv = jax.nn.silu(conv)
    qb, kb, vb = jnp.split(conv, [HB * DKB, 2 * HB * DKB], axis=-1)
    lead = qb.shape[:2]
    q = l2norm(qb.reshape(lead + (HB, DKB))) * (DKB ** -0.5)
    k = l2norm(kb.reshape(lead + (HB, DKB)))
    v = vb.reshape(lead + (HB, DVB)).astype(jnp.float32)
    beta = jax.nn.sigmoid(b_raw.astype(jnp.float32))
    g = -jnp.exp(a_log.astype(jnp.float32)) * jax.nn.softplus(a_raw.astype(jnp.float32) + dt_bias.astype(jnp.float32))
    return q, k, v, g, beta, xpad[:, t:]


def gdn_chunk(S, q, k, v, g, beta):
    q, k, v = (jnp.swapaxes(a, 1, 2) for a in (q, k, v))
    g = jnp.swapaxes(g, 1, 2)
    beta = jnp.swapaxes(beta, 1, 2)
    c = q.shape[2]
    G = jnp.cumsum(g, axis=-1)
    ti = jnp.arange(c)
    incl = ti[:, None] >= ti[None, :]
    strict = ti[:, None] > ti[None, :]
    decay = jnp.where(incl, jnp.exp(jnp.where(incl, G[..., :, None] - G[..., None, :], 0.0)), 0.0)
    kk = jnp.einsum('bhtd,bhsd->bhts', k, k)
    A = jnp.where(strict, beta[..., :, None] * decay * kk, 0.0)
    eG = jnp.exp(G)
    rhs = beta[..., None] * (v - eG[..., None] * jnp.einsum('bhtk,bhvk->bhtv', k, S))
    L = A + jnp.eye(c, dtype=A.dtype)
    delta = lax.linalg.triangular_solve(L, rhs, left_side=True, lower=True, unit_diagonal=True)
    qk = jnp.einsum('bhtd,bhsd->bhts', q, k)
    o = eG[..., None] * jnp.einsum('bhtk,bhvk->bhtv', q, S) + jnp.einsum('bhts,bhsv->bhtv', decay * qk, delta)
    Gl = G[..., -1:]
    S_new = jnp.exp(Gl)[..., None] * S + jnp.einsum('bhsv,bhsk->bhvk', delta, jnp.exp(Gl - G)[..., None] * k)
    return S_new, jnp.swapaxes(o, 1, 2)


def gdn_prompt(q, k, v, g, beta):
    b, t = q.shape[:2]
    n_chunks = (t - N_META) // CHUNK
    s0 = jnp.zeros((b, HB, DVB, DKB), jnp.float32)
    s1, o_meta = gdn_chunk(s0, q[:, :N_META], k[:, :N_META], v[:, :N_META], g[:, :N_META], beta[:, :N_META])

    def to_chunks(a):
        a = a[:, N_META:]
        return jnp.moveaxis(a.reshape((b, n_chunks, CHUNK) + a.shape[2:]), 1, 0)

    def step(s, xs):
        return gdn_chunk(s, *xs)

    s_fin, o_rest = lax.scan(step, s1, (to_chunks(q), to_chunks(k), to_chunks(v), to_chunks(g), to_chunks(beta)))
    o_rest = jnp.moveaxis(o_rest, 0, 1).reshape(b, n_chunks * CHUNK, HB, DVB)
    return jnp.concatenate([o_meta, o_rest], axis=1), s_fin


def merge_heads(oa, ob, z, lam_init, attn_subln, gdn_norm, w_out, dtype):
    b, t = oa.shape[:2]
    ya = rmsnorm(oa, attn_subln) * (1.0 - lam_init)
    yb = rmsnorm(ob, gdn_norm) * jax.nn.silu(z.reshape(b, t, HB, DVB).astype(jnp.float32))
    y = jnp.concatenate([ya.reshape(b, t, HA * DVA), yb.reshape(b, t, HB * DVB)], axis=-1).astype(dtype)
    return y @ w_out


def _top_rows(x, n, iota, nrows):
    out = []
    for _ in range(n):
        m = jnp.max(x, axis=0, keepdims=True)
        idx = jnp.min(jnp.where(x == m, iota, float(nrows)), axis=0, keepdims=True)
        out.append(m)
        x = jnp.where(iota == idx, -jnp.inf, x)
    return out


def _peer_route_kernel(x_ref, wqT_ref, keys_ref, s1_ref, s2_ref, e1_ref, e2_ref, tau_ref, qT_ref, top_ref):
    tt = x_ref.shape[0]
    qT_ref[...] = lax.dot_general(wqT_ref[...], x_ref[...], _NT, preferred_element_type=F32).astype(BF16)
    iota_k = lax.broadcasted_iota(jnp.int32, (NK, tt), 0).astype(F32)
    iota_c = lax.broadcasted_iota(jnp.int32, (NCAND, tt), 0).astype(F32)

    def head(h, carry):
        ss = []
        for c in range(2):
            row0 = pl.multiple_of((h * 2 + c) * NK, NK)
            q = qT_ref[pl.ds(row0, NK), :]
            s = jnp.dot(keys_ref[h * 2 + c], q, preferred_element_type=F32)
            ss.append(s)
            rows = _top_rows(s, PEER_TOPK, iota_k, NK)
            for r in range(PEER_TOPK):
                top_ref[c, r:r + 1, :] = rows[r]
        pieces = [top_ref[0, a:a + 1, :] + top_ref[1, 0:8, :] for a in range(8)]
        pieces.append(top_ref[0, 0:1, :] + top_ref[1, 8:16, :])
        pieces.append(top_ref[0, 8:16, :] + top_ref[1, 0:1, :])
        cand = jnp.concatenate(pieces, axis=0)
        cs = _top_rows(cand, PEER_TOPK, iota_c, NCAND)
        m = cs[0]
        z = jnp.zeros_like(m)
        for r in range(PEER_TOPK):
            z = z + jnp.exp(cs[r] - m)
        m1 = top_ref[0, 0:1, :]
        m2 = top_ref[1, 0:1, :]
        s1_ref[h] = ss[0]
        s2_ref[h] = ss[1]
        e1_ref[h] = jnp.exp(ss[0] - m1) / z
        e2_ref[h] = jnp.exp(ss[1] - m2)
        tau_ref[h] = cs[PEER_TOPK - 1]
        return carry

    lax.fori_loop(0, HP, head, 0)


def peer_route(xb, wqT, keys):
    n = xb.shape[0]
    tt = PEER_TT
    big = jax.ShapeDtypeStruct((HP, NK, n), F32)
    tok_spec = pl.BlockSpec((HP, NK, tt), lambda i: (0, 0, i))
    return pl.pallas_call(
        _peer_route_kernel,
        grid=(n // tt,),
        in_specs=[
            pl.BlockSpec((tt, D_MODEL), lambda i: (i, 0)),
            pl.BlockSpec((HP * DPK, D_MODEL), lambda i: (0, 0)),
            pl.BlockSpec((HP * 2, NK, DPK // 2), lambda i: (0, 0, 0)),
        ],
        out_specs=[tok_spec, tok_spec, tok_spec, tok_spec, pl.BlockSpec((HP, 1, tt), lambda i: (0, 0, i))],
        out_shape=[big, big, big, big, jax.ShapeDtypeStruct((HP, 1, n), F32)],
        scratch_shapes=[pltpu.VMEM((HP * DPK, tt), BF16), pltpu.VMEM((2, PEER_TOPK, tt), F32)],
        compiler_params=pltpu.CompilerParams(dimension_semantics=("arbitrary",)),
        name="peer_route",
    )(xb, wqT, keys)


def _peer_dense_kernel(x_ref, u_ref, vT_ref, s1_ref, e1_ref, s2_ref, e2_ref, tau_ref, res_ref, o_ref,
                       acc_ref, h_ref, w_ref):
    j = pl.program_id(1)
    tt = x_ref.shape[0]

    @pl.when(j == 0)
    def _():
        acc_ref[...] = jnp.zeros_like(acc_ref)

    h_ref[...] = lax.dot_general(u_ref[...], x_ref[...], _NT, preferred_element_type=F32)

    for r in range(PEER_ROWS):
        for lg in range(tt // 128):
            ls = slice(lg * 128, (lg + 1) * 128)
            g = jnp.zeros((NK, 128), F32)
            for h in range(HP):
                t = s1_ref[h, r:r + 1, ls] + s2_ref[h, :, ls]
                p = e1_ref[h, r:r + 1, ls] * e2_ref[h, :, ls]
                g = g + jnp.where(t >= tau_ref[h, :, ls], p, 0.0)
            hb = h_ref[r * NK:(r + 1) * NK, ls]
            act = 0.5 * hb * (1.0 + lax.erf(hb * (2.0 ** -0.5)))
            w_ref[r * NK:(r + 1) * NK, ls] = (g * act).astype(BF16)

    acc_ref[...] += jnp.dot(vT_ref[...], w_ref[...], preferred_element_type=F32)

    @pl.when(j == pl.num_programs(1) - 1)
    def _():
        o_ref[...] = res_ref[...] + acc_ref[...].T


def peer_dense(xb, u_bf, vT_bf, s1, s2, e1, e2, tau, res):
    n = xb.shape[0]
    tt, te = PEER_TT, PEER_TE
    row_spec = pl.BlockSpec((HP, PEER_ROWS, tt), lambda i, j: (0, j, i))
    tok_spec = pl.BlockSpec((HP, NK, tt), lambda i, j: (0, 0, i))
    return pl.pallas_call(
        _peer_dense_kernel,
        grid=(n // tt, N_EXPERTS // te),
        in_specs=[
            pl.BlockSpec((tt, D_MODEL), lambda i, j: (i, 0)),
            pl.BlockSpec((te, D_MODEL), lambda i, j: (j, 0)),
            pl.BlockSpec((D_MODEL, te), lambda i, j: (0, j)),
            row_spec, row_spec, tok_spec, tok_spec,
            pl.BlockSpec((HP, 1, tt), lambda i, j: (0, 0, i)),
            pl.BlockSpec((tt, D_MODEL), lambda i, j: (i, 0)),
        ],
        out_specs=pl.BlockSpec((tt, D_MODEL), lambda i, j: (i, 0)),
        out_shape=jax.ShapeDtypeStruct((n, D_MODEL), F32),
        scratch_shapes=[pltpu.VMEM((D_MODEL, tt), F32), pltpu.VMEM((te, tt), F32), pltpu.VMEM((te, tt), BF16)],
        compiler_params=pltpu.CompilerParams(dimension_semantics=("arbitrary", "arbitrary")),
        name="peer_dense",
    )(xb, u_bf, vT_bf, s1, e1, s2, e2, tau, res)


def peer_apply(xn, res, wq, keys, u, v):
    n = xn.shape[0]
    npad = -(-n // PEER_TT) * PEER_TT
    xb = jnp.pad(xn, ((0, npad - n), (0, 0))).astype(BF16)
    resp = jnp.pad(res, ((0, npad - n), (0, 0)))
    wqT = wq.T.astype(BF16)
    keys_b = keys.reshape(HP * 2, NK, DPK // 2).astype(BF16)
    s1, s2, e1, e2, tau = peer_route(xb, wqT, keys_b)
    y = peer_dense(xb, u.astype(BF16), v.T.astype(BF16), s1, s2, e1, e2, tau, resp)
    return y[:n]


def kernel(x_prompt, x_sample, cache_k, cache_v, state_ssm, state_conv, page_table, meta_tokens,
           norm1, w_in, conv_w, q_norm, k_norm, lam_q1, lam_k1, lam_q2, lam_k2, attn_subln,
           a_log, dt_bias, gdn_norm, w_out, norm2, peer_wq, peer_keys, peer_u, peer_v):
    f32 = jnp.float32
    slopes = jnp.exp2(-(8.0 / HA) * jnp.arange(1, HA + 1, dtype=f32))
    b = x_prompt.shape[0]
    db, ds = x_sample.shape[:2]
    meta = jnp.broadcast_to(meta_tokens.astype(x_prompt.dtype)[None], (b, N_META, D_MODEL))
    hp = jnp.concatenate([meta, x_prompt], axis=1)
    hs = x_sample
    t = hp.shape[1]
    n_past = page_table.shape[1] * PAGE_SIZE
    k_p, v_p, k_s, v_s, ssm_p, ssm_s, conv_p, conv_s = [], [], [], [], [], [], [], []
    for l in range(DEPTH):
        lam_init = 0.8 - 0.6 * math.exp(-0.3 * l)
        lam = (jnp.exp(jnp.sum(lam_q1[l].astype(f32) * lam_k1[l].astype(f32)))
               - jnp.exp(jnp.sum(lam_q2[l].astype(f32) * lam_k2[l].astype(f32))) + lam_init)

        xn = rmsnorm(hp, norm1[l])
        qa, ka, va, qkv_b, z, b_raw, a_raw = project(xn, w_in[l], q_norm[l], k_norm[l])
        oa = diff_attn_prompt(qa, ka, va, lam, slopes)
        conv0 = jnp.zeros((b, CONV_W - 1, CONV_DIM), hp.dtype)
        gq, gk, gv, gg, gbeta, conv_new = gdn_inputs(qkv_b, conv0, conv_w[l], b_raw, a_raw, a_log[l], dt_bias[l])
        ob, s_new = gdn_prompt(gq, gk, gv, gg, gbeta)
        hp = hp + merge_heads(oa, ob, z, lam_init, attn_subln[l], gdn_norm[l], w_out[l], hp.dtype)
        k_p.append(ka.reshape(b, t, HA, 2 * DQK))
        v_p.append(va)
        ssm_p.append(s_new)
        conv_p.append(conv_new)

        xn = rmsnorm(hs, norm1[l])
        qa, ka, va, qkv_b, z, b_raw, a_raw = project(xn, w_in[l], q_norm[l], k_norm[l])
        kpast = cache_k[l][page_table].reshape(db, n_past, HA, 2, DQK)
        vpast = cache_v[l][page_table].reshape(db, n_past, HA, DVA)
        k_all = jnp.concatenate([kpast.astype(ka.dtype), ka], axis=1)
        v_all = jnp.concatenate([vpast.astype(va.dtype), va], axis=1)
        qpos = n_past + jnp.arange(ds)
        kpos = jnp.arange(n_past + ds)
        oa = diff_attn_core(qa, k_all, v_all, qpos, kpos, jnp.ones((n_past + ds,), bool), lam, slopes)
        gq, gk, gv, gg, gbeta, conv_new = gdn_inputs(qkv_b, state_conv[l], conv_w[l], b_raw, a_raw, a_log[l], dt_bias[l])
        s_new, ob = gdn_chunk(state_ssm[l].astype(f32), gq, gk, gv, gg, gbeta)
        hs = hs + merge_heads(oa, ob, z, lam_init, attn_subln[l], gdn_norm[l], w_out[l], hs.dtype)
        k_s.append(ka.reshape(db, ds, HA, 2 * DQK))
        v_s.append(va)
        ssm_s.append(s_new)
        conv_s.append(conv_new)

        hall = jnp.concatenate([hp.reshape(b * t, D_MODEL), hs.reshape(db * ds, D_MODEL)], axis=0)
        hall = peer_apply(rmsnorm(hall, norm2[l]), hall, peer_wq[l], peer_keys[l], peer_u[l], peer_v[l])
        hp = hall[:b * t].reshape(b, t, D_MODEL)
        hs = hall[b * t:].reshape(db, ds, D_MODEL)

    y_prompt = hp[:, N_META:]
    y_sample = hs
    return (y_prompt, y_sample, jnp.stack(k_p), jnp.stack(v_p), jnp.stack(k_s), jnp.stack(v_s),
            jnp.stack(ssm_p), jnp.stack(ssm_s), jnp.stack(conv_p), jnp.stack(conv_s))
```

```python
import math

import jax
import jax.numpy as jnp
import numpy as np
from jax import lax
from jax.experimental import pallas as pl
from jax.experimental.pallas import tpu as pltpu

D_MODEL = 1024
DEPTH = 1
PAGE_SIZE = 128

N_META = 16
HA = 4
DQK = 64
DVA = 128
HB = 4
DKB = 128
DVB = 128
CONV_W = 4
CHUNK = 64
QA_W = HA * 2 * DQK
KA_W = HA * 2 * DQK
VA_W = HA * DVA
CONV_DIM = HB * DKB * 2 + HB * DVB
Z_W = HB * DVB
PROJ_W = QA_W + KA_W + VA_W + CONV_DIM + Z_W + 2 * HB
MIX_W = HA * DVA + HB * DVB
HP = 8
NK = 128
N_EXPERTS = NK * NK
DPK = 256
PEER_TOPK = 16
Q_BLOCK = 128
EPS = 1e-6

F32 = jnp.float32
BF16 = jnp.bfloat16

PEER_TT = 256
PEER_TE = 1024
PEER_ROWS = PEER_TE // NK
NCAND = 80

_NT = (((1,), (1,)), ((), ()))


def rmsnorm(x, g):
    xf = x.astype(jnp.float32)
    y = xf * lax.rsqrt(jnp.mean(xf * xf, axis=-1, keepdims=True) + EPS)
    return (y * g.astype(jnp.float32)).astype(x.dtype)


def l2norm(x):
    xf = x.astype(jnp.float32)
    return xf * lax.rsqrt(jnp.sum(xf * xf, axis=-1, keepdims=True) + EPS)


def project(xn, w_in, q_norm, k_norm):
    p = xn @ w_in
    cuts = np.cumsum([QA_W, KA_W, VA_W, CONV_DIM, Z_W, HB]).tolist()
    qa, ka, va, qkv_b, z, b_raw, a_raw = jnp.split(p, cuts, axis=-1)
    lead = xn.shape[:-1]
    q = rmsnorm(qa.reshape(lead + (HA, 2, DQK)), q_norm)
    k = rmsnorm(ka.reshape(lead + (HA, 2, DQK)), k_norm)
    v = va.reshape(lead + (HA, DVA))
    return q, k, v, qkv_b, z, b_raw, a_raw


def diff_attn_core(q, k, v, qpos, kpos, kvalid, lam, slopes):
    s = jnp.einsum('bqhcd,bkhcd->bhcqk', q.astype(jnp.float32), k.astype(jnp.float32)) * (DQK ** -0.5)
    dist = jnp.abs(qpos[:, None] - kpos[None, :]).astype(jnp.float32)
    s = s - slopes[:, None, None, None] * dist
    mask = (kpos[None, :] <= qpos[:, None]) & kvalid[None, :]
    p = jax.nn.softmax(jnp.where(mask, s, -jnp.inf), axis=-1)
    w = p[:, :, 0] - lam * p[:, :, 1]
    return jnp.einsum('bhqk,bkhd->bqhd', w, v.astype(jnp.float32))


ATT_T = 512
NEG = -1e30


def _attn_prompt_kernel(lam_ref, q_ref, k_ref, v_ref, o_ref, m_ref, l_ref, acc_ref):
    qi = pl.program_id(1)
    ki = pl.program_id(2)
    t = ATT_T

    @pl.when(ki == 0)
    def _():
        m_ref[...] = jnp.full_like(m_ref, NEG)
        l_ref[...] = jnp.zeros_like(l_ref)
        acc_ref[...] = jnp.zeros_like(acc_ref)

    @pl.when(ki <= qi)
    def _():
        row = lax.broadcasted_iota(jnp.int32, (t, t), 0)
        col = lax.broadcasted_iota(jnp.int32, (t, t), 1)
        dist = (row - col + (qi - ki) * t).astype(F32)
        visible = dist >= 0.0
        lane = lax.broadcasted_iota(jnp.int32, (t, 2 * DQK), 1)
        for h in range(HA):
            hs = slice(h * 2 * DQK, (h + 1) * 2 * DQK)
            bias = (2.0 ** (-2.0 * (h + 1))) * dist
            qh = q_ref[:, hs]
            kh = k_ref[:, hs]
            vh = v_ref[:, hs]
            for c in range(2):
                idx = h * 2 + c
                qc = jnp.where((lane >= c * DQK) & (lane < (c + 1) * DQK), qh, jnp.zeros_like(qh))
                s = lax.dot_general(qc, kh, _NT, preferred_element_type=F32) - bias
                s = jnp.where(visible, s, -jnp.inf)
                m_prev = m_ref[idx]
                m_new = jnp.maximum(m_prev, jnp.max(s, axis=1, keepdims=True))
                alpha = jnp.exp(m_prev - m_new)
                p = jnp.exp(s - m_new)
                l_ref[idx] = alpha * l_ref[idx] + jnp.sum(p, axis=1, keepdims=True)
                acc_ref[idx] = alpha * acc_ref[idx] + jnp.dot(p.astype(BF16), vh, preferred_element_type=F32)
                m_ref[idx] = m_new

    @pl.when(ki == qi)
    def _():
        lam = lam_ref[0]
        for h in range(HA):
            o0 = acc_ref[2 * h] / l_ref[2 * h]
            o1 = acc_ref[2 * h + 1] / l_ref[2 * h + 1]
            o_ref[:, h * DVA:(h + 1) * DVA] = o0 - lam * o1


def attn_prompt(q, k, v, lam):
    B, T = q.shape[:2]
    t = ATT_T
    nt = T // t
    w = HA * DVA
    return pl.pallas_call(
        _attn_prompt_kernel,
        grid=(B, nt, nt),
        in_specs=[
            pl.BlockSpec(memory_space=pltpu.SMEM),
            pl.BlockSpec((None, t, w), lambda b, i, j: (b, i, 0)),
            pl.BlockSpec((None, t, w), lambda b, i, j: (b, jnp.minimum(i, j), 0)),
            pl.BlockSpec((None, t, w), lambda b, i, j: (b, jnp.minimum(i, j), 0)),
        ],
        out_specs=pl.BlockSpec((None, t, w), lambda b, i, j: (b, i, 0)),
        out_shape=jax.ShapeDtypeStruct((B, T, w), F32),
        scratch_shapes=[pltpu.VMEM((2 * HA, t, 1), F32), pltpu.VMEM((2 * HA, t, 1), F32),
                        pltpu.VMEM((2 * HA, t, DVA), F32)],
        compiler_params=pltpu.CompilerParams(dimension_semantics=("arbitrary", "arbitrary", "arbitrary")),
        name="attn_prompt",
    )(lam.reshape(1), q, k, v)


def diff_attn_prompt(q, k, v, lam, slopes):
    del slopes
    B, T = q.shape[:2]
    tp = -(-T // ATT_T) * ATT_T
    pad = ((0, 0), (0, tp - T), (0, 0))
    qf = jnp.pad((q * (DQK ** -0.5)).reshape(B, T, HA * 2 * DQK), pad).astype(BF16)
    kf = jnp.pad(k.reshape(B, T, HA * 2 * DQK), pad).astype(BF16)
    vf = jnp.pad(v.reshape(B, T, HA * DVA), pad).astype(BF16)
    o = attn_prompt(qf, kf, vf, lam)
    return o[:, :T].reshape(B, T, HA, DVA)


def gdn_inputs(qkv_pre, conv_prev, conv_w, b_raw, a_raw, a_log, dt_bias):
    t = qkv_pre.shape[1]
    xpad = jnp.concatenate([conv_prev.astype(qkv_pre.dtype), qkv_pre], axis=1)
    conv = sum(conv_w[j] * xpad[:, j:j + t] for j in range(CONV_W))
    conv = jax.nn.silu(conv)
    qb, kb, vb = jnp.split(conv, [HB * DKB, 2 * HB * DKB], axis=-1)
    lead = qb.shape[:2]
    q = l2norm(qb.reshape(lead + (HB, DKB))) * (DKB ** -0.5)
    k = l2norm(kb.reshape(lead + (HB, DKB)))
    v = vb.reshape(lead + (HB, DVB)).astype(jnp.float32)
    beta = jax.nn.sigmoid(b_raw.astype(jnp.float32))
    g = -jnp.exp(a_log.astype(jnp.float32)) * jax.nn.softplus(a_raw.astype(jnp.float32) + dt_bias.astype(jnp.float32))
    return q, k, v, g, beta, xpad[:, t:]


def merge_heads(oa, ob, z, lam_init, attn_subln, gdn_norm, w_out, dtype):
    b, t = oa.shape[:2]
    ya = rmsnorm(oa, attn_subln) * (1.0 - lam_init)
    yb = rmsnorm(ob, gdn_norm) * jax.nn.silu(z.reshape(b, t, HB, DVB).astype(jnp.float32))
    y = jnp.concatenate([ya.reshape(b, t, HA * DVA), yb.reshape(b, t, HB * DVB)], axis=-1).astype(dtype)
    return y @ w_out


GDN_C = 128
GDN_NFAC = 6
GDN_GROUP = 4


def _split(x):
    hi = x.astype(BF16)
    return hi, (x - hi.astype(F32)).astype(BF16)


def _mm3(a, b):
    ah, al = _split(a)
    bh, bl = _split(b)
    d = lambda x, y: jnp.dot(x, y, preferred_element_type=F32)
    return d(ah, bh) + (d(ah, bl) + d(al, bh))


def _gdn_prep_kernel(q_ref, k_ref, v_ref, gcol_ref, bcol_ref, grow_ref, wq_ref, pk_ref, u_ref):
    c = GDN_C
    row = lax.broadcasted_iota(jnp.int32, (c, c), 0)
    col = lax.broadcasted_iota(jnp.int32, (c, c), 1)
    incl = row >= col
    strict = row > col
    for h in range(HB):
        hs = slice(h * DKB, (h + 1) * DKB)
        q = q_ref[:, hs]
        k = k_ref[:, hs]
        v = v_ref[:, hs]
        gc = gcol_ref[:, h:h + 1]
        bc = bcol_ref[:, h:h + 1]
        gr = grow_ref[h:h + 1, :]
        decay = jnp.where(incl, jnp.exp(jnp.where(incl, gc - gr, 0.0)), 0.0)
        kh, kl = _split(k)
        qh, ql = _split(q)
        d = lambda x, y: lax.dot_general(x, y, _NT, preferred_element_type=F32)
        kk = d(kh, kh) + (d(kh, kl) + d(kl, kh))
        qk = d(qh, kh) + (d(qh, kl) + d(ql, kh))
        a = jnp.where(strict, bc * decay * kk, 0.0)
        x = a
        tm = jnp.where(row == col, 1.0, 0.0) - a
        for _ in range(GDN_NFAC):
            x = _mm3(x, x)
            tm = tm + _mm3(tm, x)
        eg = jnp.exp(gc)
        rhs = jnp.concatenate([bc * v, (bc * eg) * k], axis=1)
        sol = _mm3(tm, rhs)
        u_ref[h] = sol[:, :DVB]
        wq_ref[h, 0:c, :] = sol[:, DVB:]
        wq_ref[h, c:2 * c, :] = eg * q
        pk_ref[h, 0:c, :] = decay * qk
        gl = gr[:, c - 1:c]
        pk_ref[h, c:2 * c, :] = k.T * jnp.exp(gl - gr)


def gdn_prep(q, k, v, gcol, bcol, grow):
    B, T = q.shape[:2]
    c = GDN_C
    nc = T // c
    tok = pl.BlockSpec((None, c, HB * DKB), lambda b, n: (b, n, 0))
    sc = pl.BlockSpec((None, c, HB), lambda b, n: (b, n, 0))
    return pl.pallas_call(
        _gdn_prep_kernel,
        grid=(B, nc),
        in_specs=[tok, tok, tok, sc, sc, pl.BlockSpec((None, None, HB, c), lambda b, n: (b, n, 0, 0))],
        out_specs=[
            pl.BlockSpec((None, HB, None, 2 * c, DKB), lambda b, n: (b, 0, n, 0, 0)),
            pl.BlockSpec((None, HB, None, 2 * c, c), lambda b, n: (b, 0, n, 0, 0)),
            pl.BlockSpec((None, HB, None, c, DVB), lambda b, n: (b, 0, n, 0, 0)),
        ],
        out_shape=[
            jax.ShapeDtypeStruct((B, HB, nc, 2 * c, DKB), F32),
            jax.ShapeDtypeStruct((B, HB, nc, 2 * c, c), F32),
            jax.ShapeDtypeStruct((B, HB, nc, c, DVB), F32),
        ],
        compiler_params=pltpu.CompilerParams(dimension_semantics=("arbitrary", "arbitrary")),
        name="gdn_prep",
    )(q, k, v, gcol, bcol, grow)


def _gdn_scan_kernel(wq_ref, pk_ref, u_ref, egl_ref, s0_ref, o_ref, s_ref, st_ref):
    n = pl.program_id(1)
    c = GDN_C

    @pl.when(n == 0)
    def _():
        for b in range(GDN_GROUP):
            for h in range(HB):
                st_ref[b, h] = s0_ref[b, h].T

    for b in range(GDN_GROUP):
        for h in range(HB):
            st = st_ref[b, h]
            r = _mm3(wq_ref[b, h], st)
            delta = u_ref[b, h] - r[0:c]
            r2 = _mm3(pk_ref[b, h], delta)
            o_ref[b, :, h * DVB:(h + 1) * DVB] = r[c:2 * c] + r2[0:c]
            st_ref[b, h] = egl_ref[b, h] * st + r2[c:2 * c]

    @pl.when(n == pl.num_programs(1) - 1)
    def _():
        for b in range(GDN_GROUP):
            for h in range(HB):
                s_ref[b, h] = st_ref[b, h].T


def gdn_scan(wq, pk, u, egl, s0):
    B, _, nc = wq.shape[:3]
    c = GDN_C
    g = GDN_GROUP
    big = lambda rows, cols: pl.BlockSpec((g, HB, None, rows, cols), lambda i, n: (i, 0, n, 0, 0))
    st_spec = pl.BlockSpec((g, HB, DVB, DKB), lambda i, n: (i, 0, 0, 0))
    return pl.pallas_call(
        _gdn_scan_kernel,
        grid=(B // g, nc),
        in_specs=[big(2 * c, DKB), big(2 * c, c), big(c, DVB), big(1, DVB), st_spec],
        out_specs=[pl.BlockSpec((g, c, HB * DVB), lambda i, n: (i, n, 0)), st_spec],
        out_shape=[jax.ShapeDtypeStruct((B, nc * c, HB * DVB), F32), jax.ShapeDtypeStruct((B, HB, DVB, DKB), F32)],
        scratch_shapes=[pltpu.VMEM((g, HB, DKB, DVB), F32)],
        compiler_params=pltpu.CompilerParams(dimension_semantics=("arbitrary", "arbitrary")),
        name="gdn_scan",
    )(wq, pk, u, egl, s0)


def gdn_apply(q, k, v, g, beta, s0, pad_front):
    B, T = q.shape[:2]
    c = GDN_C
    tp = -(-(T + pad_front) // c) * c
    pad = ((0, 0), (pad_front, tp - T - pad_front))
    flat = lambda a: jnp.pad(a.reshape(B, T, HB * DKB), pad + ((0, 0),))
    qf, kf, vf = flat(q), flat(k), flat(v)
    gp = jnp.pad(g, pad + ((0, 0),))
    bp = jnp.pad(beta, pad + ((0, 0),))
    nc = tp // c
    gcum = jnp.cumsum(gp.reshape(B, nc, c, HB), axis=2)
    gcol = gcum.reshape(B, tp, HB)
    grow = jnp.swapaxes(gcum, 2, 3)
    egl = jnp.exp(gcum[:, :, c - 1, :])
    egl = jnp.broadcast_to(jnp.swapaxes(egl, 1, 2)[..., None, None], (B, HB, nc, 1, DVB))
    wq, pk, u = gdn_prep(qf, kf, vf, gcol, bp, grow)
    o, s_new = gdn_scan(wq, pk, u, egl, s0)
    return o[:, pad_front:pad_front + T].reshape(B, T, HB, DVB), s_new


def _top_rows(x, n, iota, nrows):
    out = []
    for _ in range(n):
        m = jnp.max(x, axis=0, keepdims=True)
        idx = jnp.min(jnp.where(x == m, iota, float(nrows)), axis=0, keepdims=True)
        out.append(m)
        x = jnp.where(iota == idx, -jnp.inf, x)
    return out


def _peer_route_kernel(x_ref, wqT_ref, keys_ref, s1_ref, s2_ref, e1_ref, e2_ref, tau_ref, qT_ref, top_ref):
    tt = x_ref.shape[0]
    qT_ref[...] = lax.dot_general(wqT_ref[...], x_ref[...], _NT, preferred_element_type=F32).astype(BF16)
    iota_k = lax.broadcasted_iota(jnp.int32, (NK, tt), 0).astype(F32)
    iota_c = lax.broadcasted_iota(jnp.int32, (NCAND, tt), 0).astype(F32)

    def head(h, carry):
        ss = []
        for c in range(2):
            row0 = pl.multiple_of((h * 2 + c) * NK, NK)
            q = qT_ref[pl.ds(row0, NK), :]
            s = jnp.dot(keys_ref[h * 2 + c], q, preferred_element_type=F32)
            ss.append(s)
            rows = _top_rows(s, PEER_TOPK, iota_k, NK)
            for r in range(PEER_TOPK):
                top_ref[c, r:r + 1, :] = rows[r]
        pieces = [top_ref[0, a:a + 1, :] + top_ref[1, 0:8, :] for a in range(8)]
        pieces.append(top_ref[0, 0:1, :] + top_ref[1, 8:16, :])
        pieces.append(top_ref[0, 8:16, :] + top_ref[1, 0:1, :])
        cand = jnp.concatenate(pieces, axis=0)
        cs = _top_rows(cand, PEER_TOPK, iota_c, NCAND)
        m = cs[0]
        z = jnp.zeros_like(m)
        for r in range(PEER_TOPK):
            z = z + jnp.exp(cs[r] - m)
        m1 = top_ref[0, 0:1, :]
        m2 = top_ref[1, 0:1, :]
        s1_ref[h] = ss[0]
        s2_ref[h] = ss[1]
        e1_ref[h] = jnp.exp(ss[0] - m1) / z
        e2_ref[h] = jnp.exp(ss[1] - m2)
        tau_ref[h] = cs[PEER_TOPK - 1]
        return carry

    lax.fori_loop(0, HP, head, 0)


def peer_route(xb, wqT, keys):
    n = xb.shape[0]
    tt = PEER_TT
    big = jax.ShapeDtypeStruct((HP, NK, n), F32)
    tok_spec = pl.BlockSpec((HP, NK, tt), lambda i: (0, 0, i))
    return pl.pallas_call(
        _peer_route_kernel,
        grid=(n // tt,),
        in_specs=[
            pl.BlockSpec((tt, D_MODEL), lambda i: (i, 0)),
            pl.BlockSpec((HP * DPK, D_MODEL), lambda i: (0, 0)),
            pl.BlockSpec((HP * 2, NK, DPK // 2), lambda i: (0, 0, 0)),
        ],
        out_specs=[tok_spec, tok_spec, tok_spec, tok_spec, pl.BlockSpec((HP, 1, tt), lambda i: (0, 0, i))],
        out_shape=[big, big, big, big, jax.ShapeDtypeStruct((HP, 1, n), F32)],
        scratch_shapes=[pltpu.VMEM((HP * DPK, tt), BF16), pltpu.VMEM((2, PEER_TOPK, tt), F32)],
        compiler_params=pltpu.CompilerParams(dimension_semantics=("arbitrary",)),
        name="peer_route",
    )(xb, wqT, keys)


def _peer_dense_kernel(x_ref, u_ref, vT_ref, s1_ref, e1_ref, s2_ref, e2_ref, tau_ref, res_ref, o_ref,
                       acc_ref, h_ref, w_ref):
    j = pl.program_id(1)
    tt = x_ref.shape[0]

    @pl.when(j == 0)
    def _():
        acc_ref[...] = jnp.zeros_like(acc_ref)

    h_ref[...] = lax.dot_general(u_ref[...], x_ref[...], _NT, preferred_element_type=F32)

    for r in range(PEER_ROWS):
        for lg in range(tt // 128):
            ls = slice(lg * 128, (lg + 1) * 128)
            g = jnp.zeros((NK, 128), F32)
            for h in range(HP):
                t = s1_ref[h, r:r + 1, ls] + s2_ref[h, :, ls]
                p = e1_ref[h, r:r + 1, ls] * e2_ref[h, :, ls]
                g = g + jnp.where(t >= tau_ref[h, :, ls], p, 0.0)
            hb = h_ref[r * NK:(r + 1) * NK, ls]
            act = 0.5 * hb * (1.0 + lax.erf(hb * (2.0 ** -0.5)))
            w_ref[r * NK:(r + 1) * NK, ls] = (g * act).astype(BF16)

    acc_ref[...] += jnp.dot(vT_ref[...], w_ref[...], preferred_element_type=F32)

    @pl.when(j == pl.num_programs(1) - 1)
    def _():
        o_ref[...] = res_ref[...] + acc_ref[...].T


def peer_dense(xb, u_bf, vT_bf, s1, s2, e1, e2, tau, res):
    n = xb.shape[0]
    tt, te = PEER_TT, PEER_TE
    row_spec = pl.BlockSpec((HP, PEER_ROWS, tt), lambda i, j: (0, j, i))
    tok_spec = pl.BlockSpec((HP, NK, tt), lambda i, j: (0, 0, i))
    return pl.pallas_call(
        _peer_dense_kernel,
        grid=(n // tt, N_EXPERTS // te),
        in_specs=[
            pl.BlockSpec((tt, D_MODEL), lambda i, j: (i, 0)),
            pl.BlockSpec((te, D_MODEL), lambda i, j: (j, 0)),
            pl.BlockSpec((D_MODEL, te), lambda i, j: (0, j)),
            row_spec, row_spec, tok_spec, tok_spec,
            pl.BlockSpec((HP, 1, tt), lambda i, j: (0, 0, i)),
            pl.BlockSpec((tt, D_MODEL), lambda i, j: (i, 0)),
        ],
        out_specs=pl.BlockSpec((tt, D_MODEL), lambda i, j: (i, 0)),
        out_shape=jax.ShapeDtypeStruct((n, D_MODEL), F32),
        scratch_shapes=[pltpu.VMEM((D_MODEL, tt), F32), pltpu.VMEM((te, tt), F32), pltpu.VMEM((te, tt), BF16)],
        compiler_params=pltpu.CompilerParams(dimension_semantics=("arbitrary", "arbitrary")),
        name="peer_dense",
    )(xb, u_bf, vT_bf, s1, e1, s2, e2, tau, res)


def peer_apply(xn, res, wq, keys, u, v):
    n = xn.shape[0]
    npad = -(-n // PEER_TT) * PEER_TT
    xb = jnp.pad(xn, ((0, npad - n), (0, 0))).astype(BF16)
    resp = jnp.pad(res, ((0, npad - n), (0, 0)))
    wqT = wq.T.astype(BF16)
    keys_b = keys.reshape(HP * 2, NK, DPK // 2).astype(BF16)
    s1, s2, e1, e2, tau = peer_route(xb, wqT, keys_b)
    y = peer_dense(xb, u.astype(BF16), v.T.astype(BF16), s1, s2, e1, e2, tau, resp)
    return y[:n]


def kernel(x_prompt, x_sample, cache_k, cache_v, state_ssm, state_conv, page_table, meta_tokens,
           norm1, w_in, conv_w, q_norm, k_norm, lam_q1, lam_k1, lam_q2, lam_k2, attn_subln,
           a_log, dt_bias, gdn_norm, w_out, norm2, peer_wq, peer_keys, peer_u, peer_v):
    f32 = jnp.float32
    slopes = jnp.exp2(-(8.0 / HA) * jnp.arange(1, HA + 1, dtype=f32))
    b = x_prompt.shape[0]
    db, ds = x_sample.shape[:2]
    meta = jnp.broadcast_to(meta_tokens.astype(x_prompt.dtype)[None], (b, N_META, D_MODEL))
    hp = jnp.concatenate([meta, x_prompt], axis=1)
    hs = x_sample
    t = hp.shape[1]
    n_past = page_table.shape[1] * PAGE_SIZE
    k_p, v_p, k_s, v_s, ssm_p, ssm_s, conv_p, conv_s = [], [], [], [], [], [], [], []
    for l in range(DEPTH):
        lam_init = 0.8 - 0.6 * math.exp(-0.3 * l)
        lam = (jnp.exp(jnp.sum(lam_q1[l].astype(f32) * lam_k1[l].astype(f32)))
               - jnp.exp(jnp.sum(lam_q2[l].astype(f32) * lam_k2[l].astype(f32))) + lam_init)

        xn = rmsnorm(hp, norm1[l])
        qa, ka, va, qkv_b, z, b_raw, a_raw = project(xn, w_in[l], q_norm[l], k_norm[l])
        oa = diff_attn_prompt(qa, ka, va, lam, slopes)
        conv0 = jnp.zeros((b, CONV_W - 1, CONV_DIM), hp.dtype)
        gq, gk, gv, gg, gbeta, conv_new = gdn_inputs(qkv_b, conv0, conv_w[l], b_raw, a_raw, a_log[l], dt_bias[l])
        ob, s_new = gdn_apply(gq, gk, gv, gg, gbeta, jnp.zeros((b, HB, DVB, DKB), f32), GDN_C - N_META)
        hp = hp + merge_heads(oa, ob, z, lam_init, attn_subln[l], gdn_norm[l], w_out[l], hp.dtype)
        k_p.append(ka.reshape(b, t, HA, 2 * DQK))
        v_p.append(va)
        ssm_p.append(s_new)
        conv_p.append(conv_new)

        xn = rmsnorm(hs, norm1[l])
        qa, ka, va, qkv_b, z, b_raw, a_raw = project(xn, w_in[l], q_norm[l], k_norm[l])
        kpast = cache_k[l][page_table].reshape(db, n_past, HA, 2, DQK)
        vpast = cache_v[l][page_table].reshape(db, n_past, HA, DVA)
        k_all = jnp.concatenate([kpast.astype(ka.dtype), ka], axis=1)
        v_all = jnp.concatenate([vpast.astype(va.dtype), va], axis=1)
        qpos = n_past + jnp.arange(ds)
        kpos = jnp.arange(n_past + ds)
        oa = diff_attn_core(qa, k_all, v_all, qpos, kpos, jnp.ones((n_past + ds,), bool), lam, slopes)
        gq, gk, gv, gg, gbeta, conv_new = gdn_inputs(qkv_b, state_conv[l], conv_w[l], b_raw, a_raw, a_log[l], dt_bias[l])
        ob, s_new = gdn_apply(gq, gk, gv, gg, gbeta, state_ssm[l].astype(f32), 0)
        hs = hs + merge_heads(oa, ob, z, lam_init, attn_subln[l], gdn_norm[l], w_out[l], hs.dtype)
        k_s.append(ka.reshape(db, ds, HA, 2 * DQK))
        v_s.append(va)
        ssm_s.append(s_new)
        conv_s.append(conv_new)

        hall = jnp.concatenate([hp.reshape(b * t, D_MODEL), hs.reshape(db * ds, D_MODEL)], axis=0)
        hall = peer_apply(rmsnorm(hall, norm2[l]), hall, peer_wq[l], peer_keys[l], peer_u[l], peer_v[l])
        hp = hall[:b * t].reshape(b, t, D_MODEL)
        hs = hall[b * t:].reshape(db, ds, D_MODEL)

    y_prompt = hp[:, N_META:]
    y_sample = hs
    return (y_prompt, y_sample, jnp.stack(k_p), jnp.stack(v_p), jnp.stack(k_s), jnp.stack(v_s),
            jnp.stack(ssm_p), jnp.stack(ssm_s), jnp.stack(conv_p), jnp.stack(conv_s))
```

```python
import functools
import math

import jax
import jax.numpy as jnp
import numpy as np
from jax import lax
from jax.experimental import pallas as pl
from jax.experimental.pallas import tpu as pltpu

D_MODEL = 1024
DEPTH = 1
PAGE_SIZE = 128

N_META = 16
HA = 4
DQK = 64
DVA = 128
HB = 4
DKB = 128
DVB = 128
CONV_W = 4
CHUNK = 64
QA_W = HA * 2 * DQK
KA_W = HA * 2 * DQK
VA_W = HA * DVA
CONV_DIM = HB * DKB * 2 + HB * DVB
Z_W = HB * DVB
PROJ_W = QA_W + KA_W + VA_W + CONV_DIM + Z_W + 2 * HB
MIX_W = HA * DVA + HB * DVB
HP = 8
NK = 128
N_EXPERTS = NK * NK
DPK = 256
PEER_TOPK = 16
Q_BLOCK = 128
EPS = 1e-6

F32 = jnp.float32
BF16 = jnp.bfloat16

PEER_TT = 512
PEER_RT = 256
PEER_TE = 1024
PEER_ROWS = PEER_TE // NK
PEER_JB = 32
PEER_RB = 4
PEER_NSEL = PEER_TOPK + 1
TOP_ROWS = 24
NCAND = 96

_NT = (((1,), (1,)), ((), ()))


def rmsnorm(x, g):
    xf = x.astype(jnp.float32)
    y = xf * lax.rsqrt(jnp.mean(xf * xf, axis=-1, keepdims=True) + EPS)
    return (y * g.astype(jnp.float32)).astype(x.dtype)


def l2norm(x):
    xf = x.astype(jnp.float32)
    return xf * lax.rsqrt(jnp.sum(xf * xf, axis=-1, keepdims=True) + EPS)


def project(xn, w_in, q_norm, k_norm):
    p = xn @ w_in
    cuts = np.cumsum([QA_W, KA_W, VA_W, CONV_DIM, Z_W, HB]).tolist()
    qa, ka, va, qkv_b, z, b_raw, a_raw = jnp.split(p, cuts, axis=-1)
    lead = xn.shape[:-1]
    q = rmsnorm(qa.reshape(lead + (HA, 2, DQK)), q_norm)
    k = rmsnorm(ka.reshape(lead + (HA, 2, DQK)), k_norm)
    v = va.reshape(lead + (HA, DVA))
    return q, k, v, qkv_b, z, b_raw, a_raw


DEC_PAGES = 8
PAGE_ROWS = PAGE_SIZE * HA


def _attn_paged_kernel(pt_ref, lam_ref, q_ref, kn_ref, vn_ref, *rest, n_past, ds):
    del pt_ref
    kp = rest[:DEC_PAGES]
    vp = rest[DEC_PAGES:2 * DEC_PAGES]
    o_ref, qs_ref, m_ref, l_ref, acc_ref = rest[2 * DEC_PAGES:]
    j = pl.program_id(1)
    nrow = HA * 2 * ds

    @pl.when(j == 0)
    def _():
        lane = lax.broadcasted_iota(jnp.int32, (ds, 2 * DQK), 1)
        for h in range(HA):
            qh = q_ref[:, h * 2 * DQK:(h + 1) * 2 * DQK]
            for c in range(2):
                r0 = (h * 2 + c) * ds
                qs_ref[r0:r0 + ds, :] = jnp.where((lane >= c * DQK) & (lane < (c + 1) * DQK), qh, jnp.zeros_like(qh))
        m_ref[...] = jnp.full_like(m_ref, NEG)
        l_ref[...] = jnp.zeros_like(l_ref)
        acc_ref[...] = jnp.zeros_like(acc_ref)

    def masks(ncol):
        row = lax.broadcasted_iota(jnp.int32, (nrow, ncol), 0)
        col = lax.broadcasted_iota(jnp.int32, (nrow, ncol), 1)
        head = lax.shift_right_logical(row, (2 * ds).bit_length() - 1)
        tok = row & (ds - 1)
        key = lax.shift_right_logical(col, HA.bit_length() - 1)
        own = (col & (HA - 1)) == head
        slope = jnp.exp2(-2.0 * (head + 1).astype(F32))
        return own, slope, tok - key

    def update(kf, vf, own, slope, rel, kpos0, causal):
        dist = (rel + (n_past - kpos0)).astype(F32)
        s = lax.dot_general(qs_ref[...], kf, _NT, preferred_element_type=F32) - slope * dist
        keep = own & (dist >= 0.0) if causal else own
        s = jnp.where(keep, s, -jnp.inf)
        m_prev = m_ref[...]
        m_new = jnp.maximum(m_prev, jnp.max(s, axis=1, keepdims=True))
        alpha = jnp.exp(m_prev - m_new)
        p = jnp.exp(s - m_new)
        l_ref[...] = alpha * l_ref[...] + jnp.sum(p, axis=1, keepdims=True)
        acc_ref[...] = alpha * acc_ref[...] + jnp.dot(p.astype(BF16), vf, preferred_element_type=F32)
        m_ref[...] = m_new

    own, slope, rel = masks(PAGE_ROWS)
    for i in range(DEC_PAGES):
        update(kp[i][...].astype(BF16), vp[i][...].astype(BF16), own, slope, rel, (j * DEC_PAGES + i) * PAGE_SIZE, False)

    @pl.when(j == pl.num_programs(1) - 1)
    def _():
        own_n, slope_n, rel_n = masks(ds * HA)
        update(kn_ref[...], vn_ref[...], own_n, slope_n, rel_n, n_past, True)
        lam = lam_ref[0]
        for h in range(HA):
            r0 = 2 * h * ds
            o0 = acc_ref[r0:r0 + ds, :] / l_ref[r0:r0 + ds, :]
            o1 = acc_ref[r0 + ds:r0 + 2 * ds, :] / l_ref[r0 + ds:r0 + 2 * ds, :]
            o_ref[:, h * DVA:(h + 1) * DVA] = o0 - lam * o1


def attn_paged(q, kn, vn, cache_k, cache_v, page_table, lam):
    B, ds = q.shape[:2]
    n_pages = page_table.shape[1]
    assert n_pages % DEC_PAGES == 0 and ds & (ds - 1) == 0
    w = HA * DVA

    def page_spec(i):
        return pl.BlockSpec((None, PAGE_ROWS, DVA), lambda b, j, pt: (pt[b * n_pages + j * DEC_PAGES + i], 0, 0))

    grid_spec = pltpu.PrefetchScalarGridSpec(
        num_scalar_prefetch=1,
        grid=(B, n_pages // DEC_PAGES),
        in_specs=[
            pl.BlockSpec(memory_space=pltpu.SMEM),
            pl.BlockSpec((None, ds, w), lambda b, j, pt: (b, 0, 0)),
            pl.BlockSpec((None, ds * HA, DVA), lambda b, j, pt: (b, 0, 0)),
            pl.BlockSpec((None, ds * HA, DVA), lambda b, j, pt: (b, 0, 0)),
        ] + [page_spec(i) for i in range(DEC_PAGES)] + [page_spec(i) for i in range(DEC_PAGES)],
        out_specs=pl.BlockSpec((None, ds, w), lambda b, j, pt: (b, 0, 0)),
        scratch_shapes=[pltpu.VMEM((HA * 2 * ds, 2 * DQK), BF16), pltpu.VMEM((HA * 2 * ds, 1), F32),
                        pltpu.VMEM((HA * 2 * ds, 1), F32), pltpu.VMEM((HA * 2 * ds, DVA), F32)],
    )
    return pl.pallas_call(
        functools.partial(_attn_paged_kernel, n_past=n_pages * PAGE_SIZE, ds=ds),
        grid_spec=grid_spec,
        out_shape=jax.ShapeDtypeStruct((B, ds, w), F32),
        compiler_params=pltpu.CompilerParams(dimension_semantics=("arbitrary", "arbitrary")),
        name="attn_paged",
    )(page_table.reshape(-1), lam.reshape(1), q, kn, vn, *([cache_k] * DEC_PAGES), *([cache_v] * DEC_PAGES))


def diff_attn_paged(q, k, v, cache_k, cache_v, page_table, lam):
    B, ds = q.shape[:2]
    n_pool = cache_k.shape[0]
    qf = (q * (DQK ** -0.5)).reshape(B, ds, HA * 2 * DQK).astype(BF16)
    kn = k.reshape(B, ds * HA, 2 * DQK).astype(BF16)
    vn = v.reshape(B, ds * HA, DVA).astype(BF16)
    ck = cache_k.reshape(n_pool, PAGE_ROWS, 2 * DQK)
    cv = cache_v.reshape(n_pool, PAGE_ROWS, DVA)
    o = attn_paged(qf, kn, vn, ck, cv, page_table, lam)
    return o.reshape(B, ds, HA, DVA)


ATT_T = 512
NEG = -1e30


def _attn_prompt_kernel(lam_ref, qT_ref, k_ref, vT_ref, o_ref, m_ref, l_ref, acc_ref):
    qi = pl.program_id(1)
    ki = pl.program_id(2)
    t = ATT_T

    @pl.when(ki == 0)
    def _():
        m_ref[...] = jnp.full_like(m_ref, NEG)
        l_ref[...] = jnp.zeros_like(l_ref)
        acc_ref[...] = jnp.zeros_like(acc_ref)

    def step(diagonal):
        krow = lax.broadcasted_iota(jnp.int32, (t, t), 0)
        qcol = lax.broadcasted_iota(jnp.int32, (t, t), 1)
        dist = (qcol - krow + (qi - ki) * t).astype(F32)
        drow = lax.broadcasted_iota(jnp.int32, (2 * DQK, t), 0)
        for h in range(HA):
            hs = slice(h * 2 * DQK, (h + 1) * 2 * DQK)
            bias = (2.0 ** (-2.0 * (h + 1))) * dist
            qh = qT_ref[hs, :]
            kh = k_ref[:, hs]
            vh = vT_ref[hs, :]
            for c in range(2):
                idx = h * 2 + c
                qc = jnp.where((drow >= c * DQK) & (drow < (c + 1) * DQK), qh, jnp.zeros_like(qh))
                s = jnp.dot(kh, qc, preferred_element_type=F32) - bias
                if diagonal:
                    s = jnp.where(dist >= 0.0, s, -jnp.inf)
                m_prev = m_ref[idx]
                m_new = jnp.maximum(m_prev, jnp.max(s, axis=0, keepdims=True))
                alpha = jnp.exp(m_prev - m_new)
                p = jnp.exp(s - m_new)
                l_ref[idx] = alpha * l_ref[idx] + jnp.sum(p, axis=0, keepdims=True)
                acc_ref[idx] = alpha * acc_ref[idx] + jnp.dot(vh, p.astype(BF16), preferred_element_type=F32)
                m_ref[idx] = m_new

    @pl.when(ki < qi)
    def _():
        step(False)

    @pl.when(ki == qi)
    def _():
        step(True)
        lam = lam_ref[0]
        for h in range(HA):
            o0 = acc_ref[2 * h] / l_ref[2 * h]
            o1 = acc_ref[2 * h + 1] / l_ref[2 * h + 1]
            o_ref[:, h * DVA:(h + 1) * DVA] = (o0 - lam * o1).T


def attn_prompt(qT, k, vT, lam):
    B, T = k.shape[:2]
    t = ATT_T
    nt = T // t
    w = HA * DVA
    return pl.pallas_call(
        _attn_prompt_kernel,
        grid=(B, nt, nt),
        in_specs=[
            pl.BlockSpec(memory_space=pltpu.SMEM),
            pl.BlockSpec((None, w, t), lambda b, i, j: (b, 0, i)),
            pl.BlockSpec((None, t, w), lambda b, i, j: (b, jnp.minimum(i, j), 0)),
            pl.BlockSpec((None, w, t), lambda b, i, j: (b, 0, jnp.minimum(i, j))),
        ],
        out_specs=pl.BlockSpec((None, t, w), lambda b, i, j: (b, i, 0)),
        out_shape=jax.ShapeDtypeStruct((B, T, w), F32),
        scratch_shapes=[pltpu.VMEM((2 * HA, 1, t), F32), pltpu.VMEM((2 * HA, 1, t), F32),
                        pltpu.VMEM((2 * HA, DVA, t), F32)],
        compiler_params=pltpu.CompilerParams(dimension_semantics=("arbitrary", "arbitrary", "arbitrary")),
        name="attn_prompt",
    )(lam.reshape(1), qT, k, vT)


def diff_attn_prompt(q, k, v, lam):
    B, T = q.shape[:2]
    tp = -(-T // ATT_T) * ATT_T
    pad = ((0, 0), (0, tp - T), (0, 0))
    qf = jnp.pad((q * (DQK ** -0.5)).reshape(B, T, HA * 2 * DQK), pad).astype(BF16)
    kf = jnp.pad(k.reshape(B, T, HA * 2 * DQK), pad).astype(BF16)
    vf = jnp.pad(v.reshape(B, T, HA * DVA), pad).astype(BF16)
    o = attn_prompt(jnp.swapaxes(qf, 1, 2), kf, jnp.swapaxes(vf, 1, 2), lam)
    return o[:, :T].reshape(B, T, HA, DVA)


def gdn_inputs(qkv_pre, conv_prev, conv_w, b_raw, a_raw, a_log, dt_bias):
    t = qkv_pre.shape[1]
    xpad = jnp.concatenate([conv_prev.astype(qkv_pre.dtype), qkv_pre], axis=1)
    conv = sum(conv_w[j] * xpad[:, j:j + t] for j in range(CONV_W))
    conv = jax.nn.silu(conv)
    qb, kb, vb = jnp.split(conv, [HB * DKB, 2 * HB * DKB], axis=-1)
    lead = qb.shape[:2]
    q = l2norm(qb.reshape(lead + (HB, DKB))) * (DKB ** -0.5)
    k = l2norm(kb.reshape(lead + (HB, DKB)))
    v = vb.reshape(lead + (HB, DVB)).astype(jnp.float32)
    beta = jax.nn.sigmoid(b_raw.astype(jnp.float32))
    g = -jnp.exp(a_log.astype(jnp.float32)) * jax.nn.softplus(a_raw.astype(jnp.float32) + dt_bias.astype(jnp.float32))
    return q, k, v, g, beta, xpad[:, t:]


def merge_heads(oa, ob, z, lam_init, attn_subln, gdn_norm, w_out, dtype):
    b, t = oa.shape[:2]
    ya = rmsnorm(oa, attn_subln) * (1.0 - lam_init)
    yb = rmsnorm(ob, gdn_norm) * jax.nn.silu(z.reshape(b, t, HB, DVB).astype(jnp.float32))
    y = jnp.concatenate([ya.reshape(b, t, HA * DVA), yb.reshape(b, t, HB * DVB)], axis=-1).astype(dtype)
    return y @ w_out


GDN_C = 128
GDN_BASE = 16
GDN_GROUP = 4


def _split(x):
    hi = x.astype(BF16)
    return hi, (x - hi.astype(F32)).astype(BF16)


def _mm3(a, b):
    ah, al = _split(a)
    bh, bl = _split(b)
    d = lambda x, y: jnp.dot(x, y, preferred_element_type=F32)
    return d(ah, bh) + (d(ah, bl) + d(al, bh))


def _gdn_prep_kernel(q_ref, k_ref, v_ref, gcol_ref, bcol_ref, grow_ref, wq_ref, pk_ref, u_ref):
    c = GDN_C
    hb = range(HB)
    row = lax.broadcasted_iota(jnp.int32, (c, c), 0)
    col = lax.broadcasted_iota(jnp.int32, (c, c), 1)
    incl = row >= col
    strict = row > col
    eye = jnp.where(row == col, 1.0, 0.0)

    def same_block(size):
        sh = size.bit_length() - 1
        return lax.shift_right_logical(row, sh) == lax.shift_right_logical(col, sh)

    nt = lambda x, y: lax.dot_general(x, y, _NT, preferred_element_type=F32)
    q = [q_ref[:, h * DKB:(h + 1) * DKB] for h in hb]
    k = [k_ref[:, h * DKB:(h + 1) * DKB] for h in hb]
    v = [v_ref[:, h * DVB:(h + 1) * DVB] for h in hb]
    gc = [gcol_ref[:, h:h + 1] for h in hb]
    bc = [bcol_ref[:, h:h + 1] for h in hb]
    gr = [grow_ref[h:h + 1, :] for h in hb]
    decay = [jnp.where(incl, jnp.exp(jnp.where(incl, gc[h] - gr[h], 0.0)), 0.0) for h in hb]
    ks = [_split(k[h]) for h in hb]
    qs = [_split(q[h]) for h in hb]
    kk = [nt(ks[h][0], ks[h][0]) + (nt(ks[h][0], ks[h][1]) + nt(ks[h][1], ks[h][0])) for h in hb]
    qk = [nt(qs[h][0], ks[h][0]) + (nt(qs[h][0], ks[h][1]) + nt(qs[h][1], ks[h][0])) for h in hb]
    a = [jnp.where(strict, bc[h] * decay[h] * kk[h], 0.0) for h in hb]
    base = same_block(GDN_BASE)
    x = [jnp.where(base, a[h], 0.0) for h in hb]
    tm = [eye - x[h] for h in hb]
    for _ in range(GDN_BASE.bit_length() - 2):
        x = [_mm3(x[h], x[h]) for h in hb]
        tm = [tm[h] + _mm3(tm[h], x[h]) for h in hb]
    size = GDN_BASE
    while size < c:
        lower = same_block(2 * size) & jnp.logical_not(same_block(size))
        off = [jnp.where(lower, a[h], 0.0) for h in hb]
        to = [_mm3(tm[h], off[h]) for h in hb]
        tm = [tm[h] - _mm3(to[h], tm[h]) for h in hb]
        size *= 2
    eg = [jnp.exp(gc[h]) for h in hb]
    sol = [_mm3(tm[h], jnp.concatenate([bc[h] * v[h], (bc[h] * eg[h]) * k[h]], axis=1)) for h in hb]
    for h in hb:
        u_ref[h] = sol[h][:, :DVB]
        wq_ref[h, 0:c, :] = sol[h][:, DVB:]
        wq_ref[h, c:2 * c, :] = eg[h] * q[h]
        pk_ref[h, 0:c, :] = decay[h] * qk[h]
        pk_ref[h, c:2 * c, :] = k[h].T * jnp.exp(gr[h][:, c - 1:c] - gr[h])


def gdn_prep(q, k, v, gcol, bcol, grow):
    B, T = q.shape[:2]
    c = GDN_C
    nc = T // c
    tok = pl.BlockSpec((None, c, HB * DKB), lambda b, n: (b, n, 0))
    sc = pl.BlockSpec((None, c, HB), lambda b, n: (b, n, 0))
    return pl.pallas_call(
        _gdn_prep_kernel,
        grid=(B, nc),
        in_specs=[tok, tok, tok, sc, sc, pl.BlockSpec((None, None, HB, c), lambda b, n: (b, n, 0, 0))],
        out_specs=[
            pl.BlockSpec((None, HB, None, 2 * c, DKB), lambda b, n: (b, 0, n, 0, 0)),
            pl.BlockSpec((None, HB, None, 2 * c, c), lambda b, n: (b, 0, n, 0, 0)),
            pl.BlockSpec((None, HB, None, c, DVB), lambda b, n: (b, 0, n, 0, 0)),
        ],
        out_shape=[
            jax.ShapeDtypeStruct((B, HB, nc, 2 * c, DKB), F32),
            jax.ShapeDtypeStruct((B, HB, nc, 2 * c, c), F32),
            jax.ShapeDtypeStruct((B, HB, nc, c, DVB), F32),
        ],
        compiler_params=pltpu.CompilerParams(dimension_semantics=("arbitrary", "arbitrary")),
        name="gdn_prep",
    )(q, k, v, gcol, bcol, grow)


def _gdn_scan_kernel(wq_ref, pk_ref, u_ref, egl_ref, s0_ref, o_ref, s_ref, st_ref):
    n = pl.program_id(1)
    c = GDN_C

    @pl.when(n == 0)
    def _():
        for b in range(GDN_GROUP):
            for h in range(HB):
                st_ref[b, h] = s0_ref[b, h].T

    seqs = [(b, h) for b in range(GDN_GROUP) for h in range(HB)]
    st = [st_ref[b, h] for b, h in seqs]
    r = [_mm3(wq_ref[b, h], st[i]) for i, (b, h) in enumerate(seqs)]
    delta = [u_ref[b, h] - r[i][0:c] for i, (b, h) in enumerate(seqs)]
    r2 = [_mm3(pk_ref[b, h], delta[i]) for i, (b, h) in enumerate(seqs)]
    for i, (b, h) in enumerate(seqs):
        o_ref[b, :, h * DVB:(h + 1) * DVB] = r[i][c:2 * c] + r2[i][0:c]
        st_ref[b, h] = egl_ref[b, h] * st[i] + r2[i][c:2 * c]

    @pl.when(n == pl.num_programs(1) - 1)
    def _():
        for b in range(GDN_GROUP):
            for h in range(HB):
                s_ref[b, h] = st_ref[b, h].T


def gdn_scan(wq, pk, u, egl, s0):
    B, _, nc = wq.shape[:3]
    c = GDN_C
    g = GDN_GROUP
    big = lambda rows, cols: pl.BlockSpec((g, HB, None, rows, cols), lambda i, n: (i, 0, n, 0, 0))
    st_spec = pl.BlockSpec((g, HB, DVB, DKB), lambda i, n: (i, 0, 0, 0))
    return pl.pallas_call(
        _gdn_scan_kernel,
        grid=(B // g, nc),
        in_specs=[big(2 * c, DKB), big(2 * c, c), big(c, DVB), big(1, DVB), st_spec],
        out_specs=[pl.BlockSpec((g, c, HB * DVB), lambda i, n: (i, n, 0)), st_spec],
        out_shape=[jax.ShapeDtypeStruct((B, nc * c, HB * DVB), F32), jax.ShapeDtypeStruct((B, HB, DVB, DKB), F32)],
        scratch_shapes=[pltpu.VMEM((g, HB, DKB, DVB), F32)],
        compiler_params=pltpu.CompilerParams(dimension_semantics=("arbitrary", "arbitrary")),
        name="gdn_scan",
    )(wq, pk, u, egl, s0)


def gdn_apply(q, k, v, g, beta, s0, pad_front):
    B, T = q.shape[:2]
    c = GDN_C
    tp = -(-(T + pad_front) // c) * c
    pad = ((0, 0), (pad_front, tp - T - pad_front))
    flat = lambda a: jnp.pad(a.reshape(B, T, HB * DKB), pad + ((0, 0),))
    qf, kf, vf = flat(q), flat(k), flat(v)
    gp = jnp.pad(g, pad + ((0, 0),))
    bp = jnp.pad(beta, pad + ((0, 0),))
    nc = tp // c
    gcum = jnp.cumsum(gp.reshape(B, nc, c, HB), axis=2)
    gcol = gcum.reshape(B, tp, HB)
    grow = jnp.swapaxes(gcum, 2, 3)
    egl = jnp.exp(gcum[:, :, c - 1, :])
    egl = jnp.broadcast_to(jnp.swapaxes(egl, 1, 2)[..., None, None], (B, HB, nc, 1, DVB))
    wq, pk, u = gdn_prep(qf, kf, vf, gcol, bp, grow)
    o, s_new = gdn_scan(wq, pk, u, egl, s0)
    return o[:, pad_front:pad_front + T].reshape(B, T, HB, DVB), s_new


def _top_rows(x, n, iota, nrows):
    out = []
    for _ in range(n):
        m = jnp.max(x, axis=0, keepdims=True)
        idx = jnp.min(jnp.where(x == m, iota, float(nrows)), axis=0, keepdims=True)
        out.append(m)
        x = jnp.where(iota == idx, -jnp.inf, x)
    return out


def _peer_route_kernel(x_ref, wqT_ref, keys_ref, thr_ref, s2_ref, e1_ref, e2_ref, qT_ref, top_ref):
    tt = x_ref.shape[0]
    qT_ref[...] = lax.dot_general(wqT_ref[...], x_ref[...], _NT, preferred_element_type=F32).astype(BF16)
    iota_k = lax.broadcasted_iota(jnp.int32, (NK, tt), 0).astype(F32)
    iota_c = lax.broadcasted_iota(jnp.int32, (NCAND, tt), 0).astype(F32)
    top_ref[:, PEER_TOPK:TOP_ROWS, :] = jnp.full((2, TOP_ROWS - PEER_TOPK, tt), -jnp.inf, F32)

    def head(h, carry):
        ss = []
        for c in range(2):
            row0 = pl.multiple_of((h * 2 + c) * NK, NK)
            q = qT_ref[pl.ds(row0, NK), :]
            s = jnp.dot(keys_ref[h * 2 + c], q, preferred_element_type=F32)
            ss.append(s)
            rows = _top_rows(s, PEER_NSEL, iota_k, NK)
            for r in range(PEER_NSEL):
                top_ref[c, r:r + 1, :] = rows[r]
        pieces = [top_ref[0, a:a + 1, :] + top_ref[1, 0:8, :] for a in range(8)]
        pieces += [top_ref[0, 0:1, :] + top_ref[1, b:b + 8, :] for b in (8, 16)]
        pieces += [top_ref[0, a:a + 8, :] + top_ref[1, 0:1, :] for a in (8, 16)]
        cand = jnp.concatenate(pieces, axis=0)
        cs = _top_rows(cand, PEER_NSEL, iota_c, NCAND)
        m = cs[0]
        z = jnp.zeros_like(m)
        for r in range(PEER_TOPK):
            z = z + jnp.exp(cs[r] - m)
        tau = 0.5 * (cs[PEER_TOPK - 1] + cs[PEER_TOPK])
        m1 = top_ref[0, 0:1, :]
        m2 = top_ref[1, 0:1, :]
        thr_ref[h] = tau - ss[0]
        s2_ref[h] = ss[1]
        e1_ref[h] = jnp.exp(ss[0] - m1) / z
        e2_ref[h] = jnp.exp(ss[1] - m2)
        return carry

    lax.fori_loop(0, HP, head, 0)


def peer_route(xb, wqT, keys):
    n = xb.shape[0]
    tt = PEER_RT
    big = jax.ShapeDtypeStruct((HP, NK, n), F32)
    tok_spec = pl.BlockSpec((HP, NK, tt), lambda i: (0, 0, i))
    return pl.pallas_call(
        _peer_route_kernel,
        grid=(n // tt,),
        in_specs=[
            pl.BlockSpec((tt, D_MODEL), lambda i: (i, 0)),
            pl.BlockSpec((HP * DPK, D_MODEL), lambda i: (0, 0)),
            pl.BlockSpec((HP * 2, NK, DPK // 2), lambda i: (0, 0, 0)),
        ],
        out_specs=[tok_spec, tok_spec, tok_spec, tok_spec],
        out_shape=[big, big, big, big],
        scratch_shapes=[pltpu.VMEM((HP * DPK, tt), BF16), pltpu.VMEM((2, TOP_ROWS, tt), F32)],
        compiler_params=pltpu.CompilerParams(dimension_semantics=("arbitrary",)),
        name="peer_route",
    )(xb, wqT, keys)


def _peer_gate_tile(h_ref, w_ref, thr_ref, e1_ref, s2_ref, e2_ref):
    tt = h_ref.shape[1]
    nsub = PEER_JB // 8
    for lg in range(tt // 128):
        ls = slice(lg * 128, (lg + 1) * 128)
        for jb in range(NK // PEER_JB):
            for r0 in range(0, PEER_ROWS, PEER_RB):
                g = [[jnp.zeros((8, 128), F32) for _ in range(nsub)] for _ in range(PEER_RB)]
                for h in range(HP):
                    s2 = [s2_ref[h, jb * PEER_JB + 8 * k:jb * PEER_JB + 8 * (k + 1), ls] for k in range(nsub)]
                    e2 = [e2_ref[h, jb * PEER_JB + 8 * k:jb * PEER_JB + 8 * (k + 1), ls] for k in range(nsub)]
                    for r in range(PEER_RB):
                        thr = jnp.broadcast_to(thr_ref[h, r0 + r:r0 + r + 1, ls], (8, 128))
                        e1 = jnp.broadcast_to(e1_ref[h, r0 + r:r0 + r + 1, ls], (8, 128))
                        for k in range(nsub):
                            g[r][k] = g[r][k] + jnp.where(s2[k] >= thr, e1 * e2[k], 0.0)
                for r in range(PEER_RB):
                    row0 = (r0 + r) * NK + jb * PEER_JB
                    hb = h_ref[row0:row0 + PEER_JB, ls]
                    act = 0.5 * hb * (1.0 + lax.erf(hb * (2.0 ** -0.5)))
                    w_ref[row0:row0 + PEER_JB, ls] = (jnp.concatenate(g[r], axis=0) * act).astype(BF16)


def _peer_dense_kernel(xT_ref, ua_ref, ub_ref, vta_ref, vtb_ref, thra_ref, thrb_ref, e1a_ref, e1b_ref,
                       s2_ref, e2_ref, res_ref, o_ref, acc_ref, h0_ref, h1_ref, w0_ref, w1_ref):
    k = pl.program_id(1)

    @pl.when(k == 0)
    def _():
        acc_ref[...] = jnp.zeros_like(acc_ref)
        h1_ref[...] = jnp.zeros_like(h1_ref)
        w0_ref[...] = jnp.zeros_like(w0_ref)
        w1_ref[...] = jnp.zeros_like(w1_ref)

    xT = xT_ref[...]
    acc_ref[...] += jnp.dot(vta_ref[...], w0_ref[...], preferred_element_type=F32)
    _peer_gate_tile(h1_ref, w1_ref, thra_ref, e1a_ref, s2_ref, e2_ref)
    h0_ref[...] = jnp.dot(ua_ref[...], xT, preferred_element_type=F32)
    acc_ref[...] += jnp.dot(vtb_ref[...], w1_ref[...], preferred_element_type=F32)
    _peer_gate_tile(h0_ref, w0_ref, thrb_ref, e1b_ref, s2_ref, e2_ref)
    h1_ref[...] = jnp.dot(ub_ref[...], xT, preferred_element_type=F32)

    @pl.when(k == pl.num_programs(1) - 1)
    def _():
        o_ref[...] = res_ref[...] + acc_ref[...].T


def peer_dense(xT, u_bf, vT_bf, thr, s2, e1, e2, res):
    n = xT.shape[1]
    tt, te = PEER_TT, PEER_TE
    n_tiles = N_EXPERTS // te
    assert n_tiles % 2 == 0
    tile = lambda t: jnp.clip(t, 0, n_tiles - 1)
    u_spec = lambda d: pl.BlockSpec((te, D_MODEL), lambda i, k: (tile(2 * k + d), 0))
    vt_spec = lambda d: pl.BlockSpec((D_MODEL, te), lambda i, k: (0, tile(2 * k + d)))
    row_spec = lambda d: pl.BlockSpec((HP, PEER_ROWS, tt), lambda i, k: (0, tile(2 * k + d), i))
    tok_spec = pl.BlockSpec((HP, NK, tt), lambda i, k: (0, 0, i))
    return pl.pallas_call(
        _peer_dense_kernel,
        grid=(n // tt, n_tiles // 2 + 1),
        in_specs=[
            pl.BlockSpec((D_MODEL, tt), lambda i, k: (0, i)),
            u_spec(0), u_spec(1), vt_spec(-2), vt_spec(-1),
            row_spec(-1), row_spec(0), row_spec(-1), row_spec(0),
            tok_spec, tok_spec,
            pl.BlockSpec((tt, D_MODEL), lambda i, k: (i, 0)),
        ],
        out_specs=pl.BlockSpec((tt, D_MODEL), lambda i, k: (i, 0)),
        out_shape=jax.ShapeDtypeStruct((n, D_MODEL), F32),
        scratch_shapes=[pltpu.VMEM((D_MODEL, tt), F32),
                        pltpu.VMEM((te, tt), F32), pltpu.VMEM((te, tt), F32),
                        pltpu.VMEM((te, tt), BF16), pltpu.VMEM((te, tt), BF16)],
        compiler_params=pltpu.CompilerParams(dimension_semantics=("arbitrary", "arbitrary")),
        name="peer_dense",
    )(xT, u_bf, u_bf, vT_bf, vT_bf, thr, thr, e1, e1, s2, e2, res)


def peer_apply(xn, res, wq, keys, u, v):
    n = xn.shape[0]
    npad = -(-n // PEER_TT) * PEER_TT
    xb = jnp.pad(xn, ((0, npad - n), (0, 0))).astype(BF16)
    resp = jnp.pad(res, ((0, npad - n), (0, 0)))
    wqT = wq.T.astype(BF16)
    keys_b = keys.reshape(HP * 2, NK, DPK // 2).astype(BF16)
    thr, s2, e1, e2 = peer_route(xb, wqT, keys_b)
    y = peer_dense(xb.T, u.astype(BF16), v.T.astype(BF16), thr, s2, e1, e2, resp)
    return y[:n]


def kernel(x_prompt, x_sample, cache_k, cache_v, state_ssm, state_conv, page_table, meta_tokens,
           norm1, w_in, conv_w, q_norm, k_norm, lam_q1, lam_k1, lam_q2, lam_k2, attn_subln,
           a_log, dt_bias, gdn_norm, w_out, norm2, peer_wq, peer_keys, peer_u, peer_v):
    f32 = jnp.float32
    b = x_prompt.shape[0]
    db, ds = x_sample.shape[:2]
    meta = jnp.broadcast_to(meta_tokens.astype(x_prompt.dtype)[None], (b, N_META, D_MODEL))
    hp = jnp.concatenate([meta, x_prompt], axis=1)
    hs = x_sample
    t = hp.shape[1]
    n_past = page_table.shape[1] * PAGE_SIZE
    k_p, v_p, k_s, v_s, ssm_p, ssm_s, conv_p, conv_s = [], [], [], [], [], [], [], []
    for l in range(DEPTH):
        lam_init = 0.8 - 0.6 * math.exp(-0.3 * l)
        lam = (jnp.exp(jnp.sum(lam_q1[l].astype(f32) * lam_k1[l].astype(f32)))
               - jnp.exp(jnp.sum(lam_q2[l].astype(f32) * lam_k2[l].astype(f32))) + lam_init)

        xn = rmsnorm(hp, norm1[l])
        qa, ka, va, qkv_b, z, b_raw, a_raw = project(xn, w_in[l], q_norm[l], k_norm[l])
        oa = diff_attn_prompt(qa, ka, va, lam)
        conv0 = jnp.zeros((b, CONV_W - 1, CONV_DIM), hp.dtype)
        gq, gk, gv, gg, gbeta, conv_new = gdn_inputs(qkv_b, conv0, conv_w[l], b_raw, a_raw, a_log[l], dt_bias[l])
        ob, s_new = gdn_apply(gq, gk, gv, gg, gbeta, jnp.zeros((b, HB, DVB, DKB), f32), GDN_C - N_META)
        hp = hp + merge_heads(oa, ob, z, lam_init, attn_subln[l], gdn_norm[l], w_out[l], hp.dtype)
        k_p.append(ka.reshape(b, t, HA, 2 * DQK))
        v_p.append(va)
        ssm_p.append(s_new)
        conv_p.append(conv_new)

        xn = rmsnorm(hs, norm1[l])
        qa, ka, va, qkv_b, z, b_raw, a_raw = project(xn, w_in[l], q_norm[l], k_norm[l])
        oa = diff_attn_paged(qa, ka, va, cache_k[l], cache_v[l], page_table, lam)
        gq, gk, gv, gg, gbeta, conv_new = gdn_inputs(qkv_b, state_conv[l], conv_w[l], b_raw, a_raw, a_log[l], dt_bias[l])
        ob, s_new = gdn_apply(gq, gk, gv, gg, gbeta, state_ssm[l].astype(f32), 0)
        hs = hs + merge_heads(oa, ob, z, lam_init, attn_subln[l], gdn_norm[l], w_out[l], hs.dtype)
        k_s.append(ka.reshape(db, ds, HA, 2 * DQK))
        v_s.append(va)
        ssm_s.append(s_new)
        conv_s.append(conv_new)

        hall = jnp.concatenate([hp.reshape(b * t, D_MODEL), hs.reshape(db * ds, D_MODEL)], axis=0)
        hall = peer_apply(rmsnorm(hall, norm2[l]), hall, peer_wq[l], peer_keys[l], peer_u[l], peer_v[l])
        hp = hall[:b * t].reshape(b, t, D_MODEL)
        hs = hall[b * t:].reshape(db, ds, D_MODEL)

    y_prompt = hp[:, N_META:]
    y_sample = hs
    return (y_prompt, y_sample, jnp.stack(k_p), jnp.stack(v_p), jnp.stack(k_s), jnp.stack(v_s),
            jnp.stack(ssm_p), jnp.stack(ssm_s), jnp.stack(conv_p), jnp.stack(conv_s))
```
